```python
import jax
import jax.numpy as jnp
from jax import lax
import numpy as np

D_MODEL = 2048
BATCH = 4
SEQ = 4096
DEPTH = 4

GRID_W = 64
CTX_LEN = 256

MLA_HEADS = 8
Q_LORA = 768
KV_LORA = 512
QK_NOPE = 128
QK_ROPE = 64
V_HEAD = 128
ATTN_BLOCK = 128
ATTN_SCALE = (QK_NOPE + QK_ROPE) ** -0.5
ROPE_THETA = 10000.0

ML_HEADS = 4
ML_DIM = 128
ML_WIDTH = ML_HEADS * ML_DIM
ML_CHUNK = 128
CONV_W = 3

SG_GROUPS = 4
SG_DIM = 128
SG_WIDTH = SG_GROUPS * SG_DIM
SG_CHUNK = 128

MLA_WIDTH = MLA_HEADS * V_HEAD
MIX_WIDTH = MLA_WIDTH + ML_WIDTH + SG_WIDTH

N_MLA_IN = Q_LORA + KV_LORA + QK_ROPE
N_ML_IN = 4 * ML_WIDTH + 4 * ML_HEADS
N_SG_IN = 2 * SG_WIDTH
N_IN = N_MLA_IN + N_ML_IN + N_SG_IN

N_EXPERTS = 32
TOP_K = 4
D_EXPERT = 768
SWIGLU_LIMIT = 7.0
SWIGLU_ALPHA = 1.702
MOE_BLOCK = 128

NORM_EPS = 1e-6

kernel_name = 'hybrid_mla_mlstm_sgu_moe_dit'


def rmsnorm(x, g):
    xf = x.astype(jnp.float32)
    y = xf * lax.rsqrt(jnp.mean(xf * xf, axis=-1, keepdims=True) + NORM_EPS)
    return (y * g.astype(jnp.float32)).astype(x.dtype)


def ada_modulation(cond, w_ada, b_ada):
    return jnp.split(jax.nn.silu(cond) @ w_ada + b_ada, 6, axis=-1)


def grid_rope_tables(n_tokens):
    rows = n_tokens // GRID_W
    row = jnp.repeat(jnp.arange(rows), GRID_W)
    col = jnp.tile(jnp.arange(GRID_W), rows)
    n_freq = QK_ROPE // 4
    inv = ROPE_THETA ** (-jnp.arange(n_freq, dtype=jnp.float32) / n_freq)
    ra = row.astype(jnp.float32)[:, None] * inv[None, :]
    ca = col.astype(jnp.float32)[:, None] * inv[None, :]
    return (jnp.cos(ra)[:, None], jnp.sin(ra)[:, None], jnp.cos(ca)[:, None], jnp.sin(ca)[:, None])


def rotate(x, cos, sin):
    x1, x2 = jnp.split(x, 2, axis=-1)
    cos = cos.astype(x.dtype)
    sin = sin.astype(x.dtype)
    return jnp.concatenate([x1 * cos - x2 * sin, x1 * sin + x2 * cos], axis=-1)


def rope_2d(x, tabs):
    rc, rs, cc, cs = tabs
    half = QK_ROPE // 2
    return jnp.concatenate([rotate(x[..., :half], rc, rs), rotate(x[..., half:], cc, cs)], axis=-1)


def softmax_attend(q, k, v):
    s = jnp.einsum('bqhd,bkhd->bhqk', q, k, preferred_element_type=jnp.float32) * ATTN_SCALE
    p = jax.nn.softmax(s, axis=-1).astype(v.dtype)
    return jnp.einsum('bhqk,bkhd->bqhd', p, v)


def mla_project(z, tabs, with_q, p):
    B, L = z.shape[:2]
    q_a = z[..., :Q_LORA]
    c_kv = z[..., Q_LORA:Q_LORA + KV_LORA]
    k_pe = z[..., Q_LORA + KV_LORA:][:, :, None, :]
    kv = (rmsnorm(c_kv, p['kv_a_norm']) @ p['w_kv_b']).reshape(B, L, MLA_HEADS, QK_NOPE + V_HEAD)
    if tabs is not None:
        k_pe = rope_2d(k_pe, tabs)
    k = jnp.concatenate([kv[..., :QK_NOPE], jnp.broadcast_to(k_pe, (B, L, MLA_HEADS, QK_ROPE))], axis=-1)
    v = kv[..., QK_NOPE:]
    q = None
    if with_q:
        q = (rmsnorm(q_a, p['q_a_norm']) @ p['w_q_b']).reshape(B, L, MLA_HEADS, QK_NOPE + QK_ROPE)
        if tabs is not None:
            q = jnp.concatenate([q[..., :QK_NOPE], rope_2d(q[..., QK_NOPE:], tabs)], axis=-1)
    return q, k, v


def mla_mixer(z_lat, z_ctx, tabs, need_ctx, p):
    B, L = z_lat.shape[:2]
    q, k, v = mla_project(z_lat, tabs, True, p)
    qc, kc, vc = mla_project(z_ctx, None, need_ctx, p)
    k_all = jnp.concatenate([kc, k], axis=1)
    v_all = jnp.concatenate([vc, v], axis=1)
    nb = L // ATTN_BLOCK
    q_blocks = jnp.moveaxis(q.reshape(B, nb, ATTN_BLOCK, MLA_HEADS, QK_NOPE + QK_ROPE), 1, 0)
    o = lax.map(lambda qb: softmax_attend(qb, k_all, v_all), q_blocks)
    o_lat = jnp.moveaxis(o, 0, 1).reshape(B, L, MLA_WIDTH)
    o_ctx = None
    if need_ctx:
        o_ctx = softmax_attend(qc, kc, vc).reshape(B, z_ctx.shape[1], MLA_WIDTH)
    return o_lat, o_ctx


def short_conv(x, w, b):
    pad = CONV_W // 2
    L = x.shape[1]
    xp = jnp.pad(x, ((0, 0), (pad, pad), (0, 0)))
    y = b
    for j in range(CONV_W):
        y = y + xp[:, j:j + L] * w[j]
    return y


def mlstm_scan(q, k, v, log_i, log_f, state):
    B, H, L, d = q.shape
    nc = L // ML_CHUNK

    def chunks(a):
        return jnp.moveaxis(a.reshape(a.shape[:2] + (nc, ML_CHUNK) + a.shape[3:]), 2, 0)

    earlier_or_same = jnp.tril(jnp.ones((ML_CHUNK, ML_CHUNK), dtype=bool))

    def step(carry, xs):
        C, n, m = carry
        qc, kc, vc, ic, fc = xs
        b = jnp.cumsum(fc, axis=-1)
        dlog = jnp.where(earlier_or_same, b[..., :, None] - b[..., None, :] + ic[..., None, :], -jnp.inf)
        inter = b + m[..., None]
        m_t = jnp.maximum(jnp.max(dlog, axis=-1), inter)
        w = jnp.exp(dlog - m_t[..., None]) * jnp.einsum('bhtd,bhsd->bhts', qc, kc)
        w_inter = jnp.exp(inter - m_t)
        num = jnp.einsum('bhts,bhse->bhte', w, vc) + w_inter[..., None] * jnp.einsum('bhtd,bhde->bhte', qc, C)
        den = jnp.sum(w, axis=-1) + w_inter * jnp.einsum('bhtd,bhd->bht', qc, n)
        h = num / jnp.maximum(jnp.abs(den), jnp.exp(-m_t))[..., None]
        g = b[..., -1:] - b + ic
        m_new = jnp.maximum(b[..., -1] + m, jnp.max(g, axis=-1))
        decay = jnp.exp(b[..., -1] + m - m_new)
        wg = jnp.exp(g - m_new[..., None])
        C_new = decay[..., None, None] * C + jnp.einsum('bhs,bhsd,bhse->bhde', wg, kc, vc)
        n_new = decay[..., None] * n + jnp.einsum('bhs,bhsd->bhd', wg, kc)
        return (C_new, n_new, m_new), h

    state, hs = lax.scan(step, state, (chunks(q), chunks(k), chunks(v), chunks(log_i), chunks(log_f)))
    return jnp.moveaxis(hs, 0, 2).reshape(B, H, L, d), state


def mlstm_prep(z, p):
    B, L = z.shape[:2]
    W = ML_WIDTH
    qk = jax.nn.silu(short_conv(z[..., :2 * W], p['ml_conv_w'], p['ml_conv_b']))

    def heads(a):
        return jnp.swapaxes(a.reshape(B, L, ML_HEADS, ML_DIM), 1, 2).astype(jnp.float32)

    q = heads(qk[..., :W])
    k = heads(qk[..., W:]) * (ML_DIM ** -0.5)
    v = heads(z[..., 2 * W:3 * W])
    o = jax.nn.sigmoid(z[..., 3 * W:4 * W])
    g = (z[..., 4 * W:] + p['ml_gate_b']).astype(jnp.float32).reshape(B, L, 4, ML_HEADS)
    g = jnp.moveaxis(g, 1, -1)
    fwd = (g[:, 0], jax.nn.log_sigmoid(g[:, 1]))
    bwd = (g[:, 2], jax.nn.log_sigmoid(g[:, 3]))
    return q, k, v, o, fwd, bwd


def mlstm_mixer(z_lat, z_ctx, need_ctx, p):
    q, k, v, o, fwd, bwd = mlstm_prep(z_lat, p)
    qc, kc, vc, oc, fwd_c, bwd_c = mlstm_prep(z_ctx, p)
    B = q.shape[0]
    zero = (jnp.zeros((B, ML_HEADS, ML_DIM, ML_DIM), jnp.float32),
            jnp.zeros((B, ML_HEADS, ML_DIM), jnp.float32),
            jnp.zeros((B, ML_HEADS), jnp.float32))

    def flip(a):
        return jnp.flip(a, axis=2)

    h_cf, s_f = mlstm_scan(qc, kc, vc, fwd_c[0], fwd_c[1], zero)
    h_f, _ = mlstm_scan(q, k, v, fwd[0], fwd[1], s_f)
    h_cb, s_b = mlstm_scan(flip(qc), flip(kc), flip(vc), flip(bwd_c[0]), flip(bwd_c[1]), zero)
    h_b, _ = mlstm_scan(flip(q), flip(k), flip(v), flip(bwd[0]), flip(bwd[1]), s_b)

    def merge(hf, hb, og):
        h = hf + flip(hb)
        return og * jnp.swapaxes(h, 1, 2).reshape(og.shape).astype(og.dtype)

    out_lat = merge(h_f, h_b, o)
    out_ctx = merge(h_cf, h_cb, oc) if need_ctx else None
    return out_lat, out_ctx


def sgu_mixer(z, p):
    B, L = z.shape[:2]
    u = jax.nn.gelu(z[..., :SG_WIDTH])
    v = rmsnorm(jax.nn.gelu(z[..., SG_WIDTH:]), p['sg_norm'])
    v = v.reshape(B, L // SG_CHUNK, SG_CHUNK, SG_GROUPS, SG_DIM)
    mixed = jnp.einsum('gpq,bnqgc->bnpgc', p['sg_w'], v) + p['sg_b'].T[:, :, None]
    return u * mixed.reshape(B, L, SG_WIDTH)


def expert_block(xb, e, w_gu, b_gu, w_down, b_down):
    gu = xb @ w_gu[e] + b_gu[e]
    gate = jnp.minimum(gu[:, 0::2], SWIGLU_LIMIT)
    up = jnp.clip(gu[:, 1::2], -SWIGLU_LIMIT, SWIGLU_LIMIT)
    act = (up + 1.0) * (gate * jax.nn.sigmoid(SWIGLU_ALPHA * gate))
    return act @ w_down[e] + b_down[e]


def moe_ffn(h, router_w, router_b, w_gu, b_gu, w_down, b_down):
    T = h.shape[0]
    TK = T * TOP_K
    logits = jnp.matmul(h, router_w, preferred_element_type=jnp.float32) + router_b.astype(jnp.float32)
    top_val, top_exp = lax.top_k(logits, TOP_K)
    gates = jax.nn.softmax(top_val, axis=-1).astype(h.dtype)
    flat_e = top_exp.reshape(TK)
    flat_t = jnp.repeat(jnp.arange(T, dtype=jnp.int32), TOP_K)
    order = jnp.argsort(flat_e)
    sorted_e = flat_e[order]
    counts = jnp.bincount(flat_e, length=N_EXPERTS)
    padded = (counts + MOE_BLOCK - 1) // MOE_BLOCK * MOE_BLOCK
    pad_end = jnp.cumsum(padded)
    pad_start = pad_end - padded
    start = jnp.cumsum(counts) - counts
    dest = pad_start[sorted_e] + jnp.arange(TK) - start[sorted_e]
    n_blocks = -(-TK // MOE_BLOCK) + N_EXPERTS
    n_rows = n_blocks * MOE_BLOCK
    row_tok = jnp.zeros((n_rows,), jnp.int32).at[dest].set(flat_t[order])
    row_gate = jnp.zeros((n_rows,), h.dtype).at[dest].set(gates.reshape(TK)[order])
    block_exp = jnp.minimum(jnp.searchsorted(pad_end, jnp.arange(n_blocks) * MOE_BLOCK, side='right'), N_EXPERTS - 1)
    ys = lax.map(lambda a: expert_block(h[a[0]], a[1], w_gu, b_gu, w_down, b_down),
                 (row_tok.reshape(n_blocks, MOE_BLOCK), block_exp))
    return jnp.zeros_like(h).at[row_tok].add(ys.reshape(n_rows, -1) * row_gate[:, None])


def hybrid_layer(x, cx, c, c_ctx, tabs, need_ctx, p):
    B, L, D = x.shape
    sh1, sc1, g1, sh2, sc2, g2 = ada_modulation(c, p['w_ada'], p['b_ada'])
    csh1, csc1, cg1, csh2, csc2, cg2 = ada_modulation(c_ctx, p['w_ada'], p['b_ada'])
    h = rmsnorm(x, p['norm1']) * (1.0 + sc1[:, None]) + sh1[:, None]
    hc = rmsnorm(cx, p['norm1']) * (1.0 + csc1) + csh1
    z = h @ p['w_in']
    zc = hc @ p['w_in']
    a0, a1 = 0, N_MLA_IN
    m1 = a1 + N_ML_IN
    a_lat, a_ctx = mla_mixer(z[..., a0:a1], zc[..., a0:a1], tabs, need_ctx, p)
    m_lat, m_ctx = mlstm_mixer(z[..., a1:m1], zc[..., a1:m1], need_ctx, p)
    s_lat = sgu_mixer(z[..., m1:], p)
    x = x + g1[:, None] * (jnp.concatenate([a_lat, m_lat, s_lat], axis=-1) @ p['w_out'])
    h2 = rmsnorm(x, p['norm2']) * (1.0 + sc2[:, None]) + sh2[:, None]
    moe_w = (p['router_w'], p['router_b'], p['w_gu'], p['b_gu'], p['w_down'], p['b_down'])
    if not need_ctx:
        f = moe_ffn(h2.reshape(B * L, D), *moe_w)
        return x + g2[:, None] * f.reshape(B, L, D), cx
    s_ctx = sgu_mixer(zc[..., m1:], p)
    cx = cx + cg1 * (jnp.concatenate([a_ctx, m_ctx, s_ctx], axis=-1) @ p['w_out'])
    hc2 = rmsnorm(cx, p['norm2']) * (1.0 + csc2) + csh2
    f = moe_ffn(jnp.concatenate([h2.reshape(B * L, D), hc2.reshape(-1, D)], axis=0), *moe_w)
    x = x + g2[:, None] * f[:B * L].reshape(B, L, D)
    cx = cx + cg2 * f[B * L:].reshape(cx.shape)
    return x, cx


def setup_inputs(seed: int = 0) -> dict:
    key = jax.random.key(seed)
    ks = jax.random.split(key, 27)
    f32 = jnp.float32

    def nrm(k, shape, scale):
        return jax.random.normal(k, shape, f32) * scale

    def gain(k, shape):
        return 1.0 + nrm(k, shape, 0.02)

    H = ML_HEADS
    gate_base = jnp.concatenate([jnp.zeros((H,), f32), jnp.linspace(3.0, 6.0, H, dtype=f32),
                                 jnp.zeros((H,), f32), jnp.linspace(3.0, 6.0, H, dtype=f32)])
    return {
        'x': nrm(ks[0], (BATCH, SEQ, D_MODEL), 1.0),
        'c': nrm(ks[1], (BATCH, D_MODEL), 1.0),
        'ctx': nrm(ks[2], (BATCH, CTX_LEN, D_MODEL), 1.0),
        'c_ctx': nrm(ks[3], (D_MODEL,), 1.0),
        'norm1': gain(ks[4], (DEPTH, D_MODEL)),
        'norm2': gain(ks[5], (DEPTH, D_MODEL)),
        'w_ada': nrm(ks[6], (DEPTH, D_MODEL, 6 * D_MODEL), 0.5 * D_MODEL ** -0.5),
        'b_ada': nrm(ks[7], (DEPTH, 6 * D_MODEL), 0.02),
        'w_in': nrm(ks[8], (DEPTH, D_MODEL, N_IN), D_MODEL ** -0.5),
        'q_a_norm': gain(ks[9], (DEPTH, Q_LORA)),
        'w_q_b': nrm(ks[10], (DEPTH, Q_LORA, MLA_HEADS * (QK_NOPE + QK_ROPE)), Q_LORA ** -0.5),
        'kv_a_norm': gain(ks[11], (DEPTH, KV_LORA)),
        'w_kv_b': nrm(ks[12], (DEPTH, KV_LORA, MLA_HEADS * (QK_NOPE + V_HEAD)), KV_LORA ** -0.5),
        'ml_conv_w': nrm(ks[13], (DEPTH, CONV_W, 2 * ML_WIDTH), CONV_W ** -0.5),
        'ml_conv_b': nrm(ks[14], (DEPTH, 2 * ML_WIDTH), 0.02),
        'ml_gate_b': gate_base + nrm(ks[15], (DEPTH, 4 * ML_HEADS), 0.1),
        'sg_norm': gain(ks[16], (DEPTH, SG_WIDTH)),
        'sg_w': nrm(ks[17], (DEPTH, SG_GROUPS, SG_CHUNK, SG_CHUNK), SG_CHUNK ** -0.5),
        'sg_b': nrm(ks[18], (DEPTH, SG_GROUPS, SG_CHUNK), 0.02),
        'w_out': nrm(ks[19], (DEPTH, MIX_WIDTH, D_MODEL), MIX_WIDTH ** -0.5),
        'router_w': nrm(ks[20], (DEPTH, D_MODEL, N_EXPERTS), D_MODEL ** -0.5),
        'router_b': nrm(ks[21], (DEPTH, N_EXPERTS), 0.01),
        'w_gu': nrm(ks[22], (DEPTH, N_EXPERTS, D_MODEL, 2 * D_EXPERT), D_MODEL ** -0.5),
        'b_gu': nrm(ks[23], (DEPTH, N_EXPERTS, 2 * D_EXPERT), 0.02),
        'w_down': nrm(ks[24], (DEPTH, N_EXPERTS, D_EXPERT, D_MODEL), D_EXPERT ** -0.5),
        'b_down': nrm(ks[25], (DEPTH, N_EXPERTS, D_MODEL), 0.02),
        'final_norm': gain(ks[26], (D_MODEL,)),
    }


def reference(x, c, ctx, c_ctx, norm1, norm2, w_ada, b_ada, w_in, q_a_norm, w_q_b, kv_a_norm, w_kv_b,
              ml_conv_w, ml_conv_b, ml_gate_b, sg_norm, sg_w, sg_b, w_out, router_w, router_b,
              w_gu, b_gu, w_down, b_down, final_norm):
    tabs = grid_rope_tables(x.shape[1])
    cx = ctx
    for l in range(DEPTH):
        p = dict(norm1=norm1[l], norm2=norm2[l], w_ada=w_ada[l], b_ada=b_ada[l], w_in=w_in[l],
                 q_a_norm=q_a_norm[l], w_q_b=w_q_b[l], kv_a_norm=kv_a_norm[l], w_kv_b=w_kv_b[l],
                 ml_conv_w=ml_conv_w[l], ml_conv_b=ml_conv_b[l], ml_gate_b=ml_gate_b[l],
                 sg_norm=sg_norm[l], sg_w=sg_w[l], sg_b=sg_b[l], w_out=w_out[l],
                 router_w=router_w[l], router_b=router_b[l], w_gu=w_gu[l], b_gu=b_gu[l],
                 w_down=w_down[l], b_down=b_down[l])
        x, cx = hybrid_layer(x, cx, c, c_ctx, tabs, l < DEPTH - 1, p)
    return rmsnorm(x, final_norm)
```

```python
import functools

import jax
import jax.numpy as jnp
from jax import lax
from jax.experimental import pallas as pl
from jax.experimental.pallas import tpu as pltpu

F32 = jnp.float32
BF16 = jnp.bfloat16
I32 = jnp.int32

GRID_W = 64

MLA_HEADS = 8
Q_LORA = 768
KV_LORA = 512
QK_NOPE = 128
QK_ROPE = 64
V_HEAD = 128
ATTN_SCALE = (QK_NOPE + QK_ROPE) ** -0.5
ROPE_THETA = 10000.0
QK_PAD = 256

ML_HEADS = 4
ML_DIM = 128
ML_WIDTH = ML_HEADS * ML_DIM
ML_CHUNK = 128

SG_GROUPS = 4
SG_DIM = 128
SG_WIDTH = SG_GROUPS * SG_DIM
SG_CHUNK = 128

MLA_WIDTH = MLA_HEADS * V_HEAD

N_EXPERTS = 32
TOP_K = 4
D_EXPERT = 768
SWIGLU_LIMIT = 7.0
SWIGLU_ALPHA = 1.702

NORM_EPS = 1e-6
LANES = 128
NEG_BIG = -1e30

Z_QA = 0
Z_CKV = 768
Z_KPE = 1280
Z_GATE = 1408
Z_MLQ = 1536
Z_MLK = 2048
Z_MLV = 2560
Z_MLO = 3072
Z_SGU = 3584
Z_SGV = 4096
Z_WIDTH = 4608

VMEM_LIMIT = 56 * 1024 * 1024


def _cparams(*sem):
    return pltpu.CompilerParams(dimension_semantics=sem, vmem_limit_bytes=VMEM_LIMIT)


def _pick(n, candidates):
    for c in candidates:
        if n % c == 0:
            return c
    raise ValueError(f"no tile for {n} in {candidates}")


def _rms(x, g):
    return x * lax.rsqrt(jnp.mean(x * x, axis=-1, keepdims=True) + NORM_EPS) * g


def _ada_kernel(cond_ref, w_ref, b_ref, o_ref):
    cnd = cond_ref[...]
    s = (cnd * jax.nn.sigmoid(cnd)).astype(BF16)
    o_ref[0] = jnp.dot(s, w_ref[0].astype(BF16), preferred_element_type=F32) + b_ref[0]


def ada_all(cond, w_ada, b_ada):
    depth, d, n = w_ada.shape
    tn = _pick(n, (512, 256, 128))
    rows = cond.shape[0]
    return pl.pallas_call(
        _ada_kernel,
        out_shape=jax.ShapeDtypeStruct((depth, rows, n), F32),
        grid=(depth, n // tn),
        in_specs=[
            pl.BlockSpec((rows, d), lambda l, j: (0, 0)),
            pl.BlockSpec((1, d, tn), lambda l, j: (l, 0, j)),
            pl.BlockSpec((1, 1, tn), lambda l, j: (l, 0, j)),
        ],
        out_specs=pl.BlockSpec((1, rows, tn), lambda l, j: (l, 0, j)),
        compiler_params=_cparams("parallel", "parallel"),
        name="ada",
    )(cond, w_ada, b_ada.reshape(depth, 1, n))


def _in_proj_kernel(x_ref, mod_ref, g_ref, w_ref, z_ref, h_scr):
    @pl.when(pl.program_id(1) == 0)
    def _():
        h = _rms(x_ref[...], g_ref[...]) * (1.0 + mod_ref[1:2, :]) + mod_ref[0:1, :]
        h_scr[...] = h.astype(BF16)

    z_ref[...] = jnp.dot(h_scr[...], w_ref[...], preferred_element_type=F32)


def in_proj(x, mod, g, w, tiles_per_batch_of, n_batch):
    t, d = x.shape
    n = w.shape[1]
    tm = tiles_per_batch_of["tm"]
    tpb = tiles_per_batch_of["tpb"]
    tn = 512
    return pl.pallas_call(
        _in_proj_kernel,
        out_shape=jax.ShapeDtypeStruct((t, n), F32),
        grid=(t // tm, n // tn),
        in_specs=[
            pl.BlockSpec((tm, d), lambda i, j: (i, 0)),
            pl.BlockSpec((None, 6, d), lambda i, j: (jnp.minimum(i // tpb, n_batch), 0, 0)),
            pl.BlockSpec((1, d), lambda i, j: (0, 0)),
            pl.BlockSpec((d, tn), lambda i, j: (0, j)),
        ],
        out_specs=pl.BlockSpec((tm, tn), lambda i, j: (i, j)),
        scratch_shapes=[pltpu.VMEM((tm, d), BF16)],
        compiler_params=_cparams("parallel", "arbitrary"),
        name="in_proj",
    )(x, mod, g, w)


def _mla_proj_kernel(z_ref, qg_ref, kg_ref, wq_ref, wkv_ref, cs_ref, sn_ref, q_ref, k_ref, v_ref):
    cs = cs_ref[...]
    sn = sn_ref[...]
    qn = _rms(z_ref[:, Z_QA:Z_QA + Q_LORA], qg_ref[...]).astype(BF16)
    kn = _rms(z_ref[:, Z_CKV:Z_CKV + KV_LORA], kg_ref[...]).astype(BF16)
    kpe = z_ref[:, Z_KPE:Z_KPE + LANES]
    k_rot = (kpe * cs + pltpu.roll(kpe, LANES // 2, axis=1) * sn).astype(BF16)
    qf = jnp.dot(qn, wq_ref[...], preferred_element_type=F32)
    kv = jnp.dot(kn, wkv_ref[...], preferred_element_type=F32)
    for h in range(MLA_HEADS):
        qb = h * 3 * LANES
        q_ref[:, h * QK_PAD:h * QK_PAD + LANES] = qf[:, qb:qb + LANES].astype(BF16)
        q_rot = qf[:, qb + LANES:qb + 2 * LANES] * cs + qf[:, qb + 2 * LANES:qb + 3 * LANES] * sn
        q_ref[:, h * QK_PAD + LANES:(h + 1) * QK_PAD] = q_rot.astype(BF16)
        kb = h * (QK_NOPE + V_HEAD)
        k_ref[:, h * QK_PAD:h * QK_PAD + LANES] = kv[:, kb:kb + QK_NOPE].astype(BF16)
        k_ref[:, h * QK_PAD + LANES:(h + 1) * QK_PAD] = k_rot
        v_ref[:, h * V_HEAD:(h + 1) * V_HEAD] = kv[:, kb + QK_NOPE:kb + QK_NOPE + V_HEAD].astype(BF16)


def mla_proj(z, qg, kg, wq, wkv, cs, sn):
    t = z.shape[0]
    tm = _pick(t, (256, 128))
    zw = Z_MLQ
    return pl.pallas_call(
        _mla_proj_kernel,
        out_shape=(
            jax.ShapeDtypeStruct((t, MLA_HEADS * QK_PAD), BF16),
            jax.ShapeDtypeStruct((t, MLA_HEADS * QK_PAD), BF16),
            jax.ShapeDtypeStruct((t, MLA_HEADS * V_HEAD), BF16),
        ),
        grid=(t // tm,),
        in_specs=[
            pl.BlockSpec((tm, zw), lambda i: (i, 0)),
            pl.BlockSpec((1, Q_LORA), lambda i: (0, 0)),
            pl.BlockSpec((1, KV_LORA), lambda i: (0, 0)),
            pl.BlockSpec(wq.shape, lambda i: (0, 0)),
            pl.BlockSpec(wkv.shape, lambda i: (0, 0)),
            pl.BlockSpec((tm, LANES), lambda i: (i, 0)),
            pl.BlockSpec((tm, LANES), lambda i: (i, 0)),
        ],
        out_specs=(
            pl.BlockSpec((tm, MLA_HEADS * QK_PAD), lambda i: (i, 0)),
            pl.BlockSpec((tm, MLA_HEADS * QK_PAD), lambda i: (i, 0)),
            pl.BlockSpec((tm, MLA_HEADS * V_HEAD), lambda i: (i, 0)),
        ),
        compiler_params=_cparams("parallel"),
        name="mla_proj",
    )(z, qg, kg, wq, wkv, cs, sn)


_NT = (((1,), (1,)), ((), ()))


def _attn_lat_kernel(q_ref, kl_ref, kc_ref, vl_ref, vc_ref, o_ref):
    q = q_ref[...]
    s1 = lax.dot_general(q, kl_ref[...], _NT, preferred_element_type=F32) * ATTN_SCALE
    s2 = lax.dot_general(q, kc_ref[...], _NT, preferred_element_type=F32) * ATTN_SCALE
    m = jnp.maximum(jnp.max(s1, axis=-1, keepdims=True), jnp.max(s2, axis=-1, keepdims=True))
    p1 = jnp.exp(s1 - m)
    p2 = jnp.exp(s2 - m)
    den = jnp.sum(p1, axis=-1, keepdims=True) + jnp.sum(p2, axis=-1, keepdims=True)
    o = (jnp.dot(p1.astype(BF16), vl_ref[...], preferred_element_type=F32)
         + jnp.dot(p2.astype(BF16), vc_ref[...], preferred_element_type=F32))
    o_ref[...] = (o / den).astype(BF16)


def attn_latent(q, k, v, n_batch, seq, ctx_len):
    t_lat = n_batch * seq
    tq = _pick(seq, (256, 128))
    nq = seq // tq
    ctx_blk0 = t_lat // ctx_len
    return pl.pallas_call(
        _attn_lat_kernel,
        out_shape=jax.ShapeDtypeStruct((t_lat, MLA_WIDTH), BF16),
        grid=(n_batch, MLA_HEADS, nq),
        in_specs=[
            pl.BlockSpec((tq, QK_PAD), lambda b, h, i: (b * nq + i, h)),
            pl.BlockSpec((seq, QK_PAD), lambda b, h, i: (b, h)),
            pl.BlockSpec((ctx_len, QK_PAD), lambda b, h, i: (ctx_blk0 + b, h)),
            pl.BlockSpec((seq, V_HEAD), lambda b, h, i: (b, h)),
            pl.BlockSpec((ctx_len, V_HEAD), lambda b, h, i: (ctx_blk0 + b, h)),
        ],
        out_specs=pl.BlockSpec((tq, V_HEAD), lambda b, h, i: (b * nq + i, h)),
        compiler_params=_cparams("parallel", "parallel", "parallel"),
        name="attn_latent",
    )(q, k, k, v, v)


def _attn_ctx_kernel(q_ref, k_ref, v_ref, o_ref):
    s = lax.dot_general(q_ref[...], k_ref[...], _NT, preferred_element_type=F32) * ATTN_SCALE
    p = jnp.exp(s - jnp.max(s, axis=-1, keepdims=True))
    den = jnp.sum(p, axis=-1, keepdims=True)
    o = jnp.dot(p.astype(BF16), v_ref[...], preferred_element_type=F32)
    o_ref[...] = (o / den).astype(BF16)


def attn_context(q, k, v, n_batch, seq, ctx_len):
    ctx_blk0 = n_batch * seq // ctx_len
    return pl.pallas_call(
        _attn_ctx_kernel,
        out_shape=jax.ShapeDtypeStruct((n_batch * ctx_len, MLA_WIDTH), BF16),
        grid=(n_batch, MLA_HEADS),
        in_specs=[
            pl.BlockSpec((ctx_len, QK_PAD), lambda b, h: (ctx_blk0 + b, h)),
            pl.BlockSpec((ctx_len, QK_PAD), lambda b, h: (ctx_blk0 + b, h)),
            pl.BlockSpec((ctx_len, V_HEAD), lambda b, h: (ctx_blk0 + b, h)),
        ],
        out_specs=pl.BlockSpec((ctx_len, V_HEAD), lambda b, h: (b, h)),
        compiler_params=_cparams("parallel", "parallel"),
        name="attn_context",
    )(q, k, v)


def _ml_prep_kernel(z_ref, prev_ref, next_ref, w_ref, b_ref, o_ref, *, tm, seq, ctx_len, t_lat):
    i = pl.program_id(0)
    j = pl.program_id(1)
    r0 = i * tm
    in_lat = r0 < t_lat
    first = jnp.where(in_lat, r0 % seq == 0, (r0 - t_lat) % ctx_len == 0)
    last = jnp.where(in_lat, (r0 + tm) % seq == 0, (r0 + tm - t_lat) % ctx_len == 0)
    x = z_ref[...]
    prev = prev_ref[7:8, :] * jnp.where(first, 0.0, 1.0)
    nxt = next_ref[0:1, :] * jnp.where(last, 0.0, 1.0)
    rows = lax.broadcasted_iota(I32, x.shape, 0)
    xm = jnp.where(rows == 0, prev, pltpu.roll(x, 1, axis=0))
    xp = jnp.where(rows == tm - 1, nxt, pltpu.roll(x, tm - 1, axis=0))
    y = b_ref[...] + xm * w_ref[0:1, :] + x * w_ref[1:2, :] + xp * w_ref[2:3, :]
    y = y * jax.nn.sigmoid(y)
    o_ref[...] = (y * jnp.where(j == 1, ML_DIM ** -0.5, 1.0)).astype(BF16)


def ml_prep(z, conv_w, conv_b, seq, ctx_len, t_lat):
    t = z.shape[0]
    tm = _pick(ctx_len, (256, 128))
    cb = Z_MLQ // ML_WIDTH
    nb8 = t // 8
    kern = functools.partial(_ml_prep_kernel, tm=tm, seq=seq, ctx_len=ctx_len, t_lat=t_lat)
    return pl.pallas_call(
        kern,
        out_shape=jax.ShapeDtypeStruct((t, 2 * ML_WIDTH), BF16),
        grid=(t // tm, 2),
        in_specs=[
            pl.BlockSpec((tm, ML_WIDTH), lambda i, j: (i, cb + j)),
            pl.BlockSpec((8, ML_WIDTH), lambda i, j: (jnp.maximum(i * (tm // 8) - 1, 0), cb + j)),
            pl.BlockSpec((8, ML_WIDTH), lambda i, j: (jnp.minimum((i + 1) * (tm // 8), nb8 - 1), cb + j)),
            pl.BlockSpec((3, ML_WIDTH), lambda i, j: (0, j)),
            pl.BlockSpec((1, ML_WIDTH), lambda i, j: (0, j)),
        ],
        out_specs=pl.BlockSpec((tm, ML_WIDTH), lambda i, j: (i, j)),
        compiler_params=_cparams("parallel", "parallel"),
        name="ml_prep",
    )(z, z, z, conv_w, conv_b)


def _log_sigmoid(x):
    return jnp.minimum(x, 0.0) - jnp.log1p(jnp.exp(-jnp.abs(x)))


def _split3(x):
    hi = x.astype(BF16)
    r = x - hi.astype(F32)
    mid = r.astype(BF16)
    lo = (r - mid.astype(F32)).astype(BF16)
    return hi, mid, lo


def _mlstm_kernel(qkf_ref, vf_ref, gf_ref, qkb_ref, vb_ref, gb_ref, gbias_ref,
                  hf_ref, hb_ref, c_scr, n_scr, m_scr):
    @pl.when(pl.program_id(1) == 0)
    def _():
        c_scr[...] = jnp.zeros_like(c_scr)
        n_scr[...] = jnp.zeros_like(n_scr)
        m_scr[...] = jnp.zeros_like(m_scr)

    ch = ML_CHUNK
    row = lax.broadcasted_iota(I32, (ch, ch), 0)
    col = lax.broadcasted_iota(I32, (ch, ch), 1)
    dirs = (
        (qkf_ref, vf_ref, gf_ref, hf_ref, col <= row, ch - 1),
        (qkb_ref, vb_ref, gb_ref, hb_ref, col >= row, 0),
    )
    for d, (qk_ref, v_ref, g_ref, h_ref, mask, end) in enumerate(dirs):
        g = g_ref[...] + gbias_ref[...]
        lf = _log_sigmoid(g)
        tri = jnp.where(mask, 1.0, 0.0).astype(BF16)
        hi, mid, lo = _split3(lf)
        bcol = (jnp.dot(tri, hi, preferred_element_type=F32)
                + jnp.dot(tri, mid, preferred_element_type=F32)
                + jnp.dot(tri, lo, preferred_element_type=F32))
        brow = bcol.T
        irow = g.T
        for h in range(ML_HEADS):
            chain = d * ML_HEADS + h
            gi = d * 2 * ML_HEADS + h
            gf = gi + ML_HEADS
            b_c = bcol[:, gf:gf + 1]
            b_r = brow[gf:gf + 1, :]
            i_r = irow[gi:gi + 1, :]
            i_c = g[:, gi:gi + 1]
            m_prev = m_scr[chain:chain + 1, 0:1]
            n_prev = n_scr[chain:chain + 1, :]
            c_prev = c_scr[chain]
            q = qk_ref[:, h * ML_DIM:(h + 1) * ML_DIM]
            k = qk_ref[:, ML_WIDTH + h * ML_DIM:ML_WIDTH + (h + 1) * ML_DIM]
            v = v_ref[:, h * ML_DIM:(h + 1) * ML_DIM].astype(BF16)

            dlog = jnp.where(mask, b_c - b_r + i_r, -jnp.inf)
            inter = b_c + m_prev
            m_t = jnp.maximum(jnp.max(dlog, axis=-1, keepdims=True), inter)
            w = jnp.exp(dlog - m_t) * lax.dot_general(q, k, _NT, preferred_element_type=F32)
            w_inter = jnp.exp(inter - m_t)
            num = (jnp.dot(w.astype(BF16), v, preferred_element_type=F32)
                   + w_inter * jnp.dot(q, c_prev.astype(BF16), preferred_element_type=F32))
            den = (jnp.sum(w, axis=-1, keepdims=True)
                   + w_inter * jnp.sum(q.astype(F32) * n_prev, axis=-1, keepdims=True))
            h_ref[:, h * ML_DIM:(h + 1) * ML_DIM] = num / jnp.maximum(jnp.abs(den), jnp.exp(-m_t))

            b_tot = b_c[end:end + 1, :]
            g_c = b_tot - b_c + i_c
            m_new = jnp.maximum(b_tot + m_prev, jnp.max(g_c, axis=0, keepdims=True))
            decay = jnp.exp(b_tot + m_prev - m_new)
            kw = k.astype(F32) * jnp.exp(g_c - m_new)
            c_scr[chain] = decay * c_prev + jnp.dot(kw.T.astype(BF16), v, preferred_element_type=F32)
            n_scr[chain:chain + 1, :] = decay * n_prev + jnp.sum(kw, axis=0, keepdims=True)
            m_scr[chain:chain + 1, :] = jnp.broadcast_to(m_new, (1, LANES))


def mlstm(qk, z, gate_b, n_batch, seq, ctx_len):
    t = z.shape[0]
    ch = ML_CHUNK
    ncc = ctx_len // ch
    ncl = seq // ch
    ctx0 = n_batch * seq // ch

    def fwd_blk(b, c):
        return jnp.where(c < ncc, ctx0 + b * ncc + c, b * ncl + (c - ncc))

    def bwd_blk(b, c):
        return jnp.where(c < ncc, ctx0 + b * ncc + (ncc - 1 - c), b * ncl + (ncl - 1 - (c - ncc)))

    vcol = Z_MLV // ML_WIDTH
    gcol = Z_GATE // LANES
    n_chain = 2 * ML_HEADS
    return pl.pallas_call(
        _mlstm_kernel,
        out_shape=(jax.ShapeDtypeStruct((t, ML_WIDTH), F32), jax.ShapeDtypeStruct((t, ML_WIDTH), F32)),
        grid=(n_batch, ncc + ncl),
        in_specs=[
            pl.BlockSpec((ch, 2 * ML_WIDTH), lambda b, c: (fwd_blk(b, c), 0)),
            pl.BlockSpec((ch, ML_WIDTH), lambda b, c: (fwd_blk(b, c), vcol)),
            pl.BlockSpec((ch, LANES), lambda b, c: (fwd_blk(b, c), gcol)),
            pl.BlockSpec((ch, 2 * ML_WIDTH), lambda b, c: (bwd_blk(b, c), 0)),
            pl.BlockSpec((ch, ML_WIDTH), lambda b, c: (bwd_blk(b, c), vcol)),
            pl.BlockSpec((ch, LANES), lambda b, c: (bwd_blk(b, c), gcol)),
            pl.BlockSpec((1, LANES), lambda b, c: (0, 0)),
        ],
        out_specs=(
            pl.BlockSpec((ch, ML_WIDTH), lambda b, c: (fwd_blk(b, c), 0)),
            pl.BlockSpec((ch, ML_WIDTH), lambda b, c: (bwd_blk(b, c), 0)),
        ),
        scratch_shapes=[
            pltpu.VMEM((n_chain, ML_DIM, ML_DIM), F32),
            pltpu.VMEM((n_chain, ML_DIM), F32),
            pltpu.VMEM((n_chain, LANES), F32),
        ],
        compiler_params=_cparams("parallel", "arbitrary"),
        name="mlstm",
    )(qk, z, z, qk, z, z, gate_b)


def _sgu_kernel(u_ref, v_ref, g_ref, w_ref, b_ref, o_ref, *, tm):
    u = jax.nn.gelu(u_ref[...])
    vn = _rms(jax.nn.gelu(v_ref[...]), g_ref[...]).astype(BF16)
    for n in range(tm // SG_CHUNK):
        r = slice(n * SG_CHUNK, (n + 1) * SG_CHUNK)
        for gi in range(SG_GROUPS):
            c = slice(gi * SG_DIM, (gi + 1) * SG_DIM)
            mixed = jnp.dot(w_ref[gi], vn[r, c], preferred_element_type=F32) + b_ref[gi]
            o_ref[r, c] = (u[r, c] * mixed).astype(BF16)


def sgu(z, g, w, b_bc):
    t = z.shape[0]
    tm = _pick(t, (512, 256, 128))
    return pl.pallas_call(
        functools.partial(_sgu_kernel, tm=tm),
        out_shape=jax.ShapeDtypeStruct((t, SG_WIDTH), BF16),
        grid=(t // tm,),
        in_specs=[
            pl.BlockSpec((tm, SG_WIDTH), lambda i: (i, Z_SGU // SG_WIDTH)),
            pl.BlockSpec((tm, SG_WIDTH), lambda i: (i, Z_SGV // SG_WIDTH)),
            pl.BlockSpec((1, SG_WIDTH), lambda i: (0, 0)),
            pl.BlockSpec(w.shape, lambda i: (0, 0, 0)),
            pl.BlockSpec(b_bc.shape, lambda i: (0, 0, 0)),
        ],
        out_specs=pl.BlockSpec((tm, SG_WIDTH), lambda i: (i, 0)),
        compiler_params=_cparams("parallel"),
        name="sgu",
    )(z, z, g, w, b_bc)


def _out_proj_kernel(a_ref, zo_ref, hf_ref, hb_ref, s_ref, x_ref, mod_ref, w_ref, g2_ref, rw_ref, rb_ref,
                     xo_ref, h2_ref, eid_ref, gate_ref, cnt_ref):
    @pl.when(pl.program_id(0) == 0)
    def _():
        cnt_ref[...] = jnp.zeros_like(cnt_ref)

    ml = (jax.nn.sigmoid(zo_ref[...]) * (hf_ref[...] + hb_ref[...])).astype(BF16)
    y = (jnp.dot(a_ref[...], w_ref[0:MLA_WIDTH, :], preferred_element_type=F32)
         + jnp.dot(ml, w_ref[MLA_WIDTH:MLA_WIDTH + ML_WIDTH, :], preferred_element_type=F32)
         + jnp.dot(s_ref[...], w_ref[MLA_WIDTH + ML_WIDTH:, :], preferred_element_type=F32))
    xn = x_ref[...] + mod_ref[2:3, :] * y
    xo_ref[...] = xn
    h2 = _rms(xn, g2_ref[...]) * (1.0 + mod_ref[4:5, :]) + mod_ref[3:4, :]
    h2_ref[...] = h2

    logits = jnp.dot(h2.astype(BF16), rw_ref[...], preferred_element_type=F32) + rb_ref[...]
    lane = lax.broadcasted_iota(I32, logits.shape, 1).astype(F32)
    eid = jnp.zeros_like(logits)
    ex = jnp.zeros_like(logits)
    hits = jnp.zeros_like(logits)
    top = None
    for kk in range(TOP_K):
        mx = jnp.max(logits, axis=-1, keepdims=True)
        idx = jnp.min(jnp.where(logits == mx, lane, float(LANES)), axis=-1, keepdims=True)
        sel = lane == idx
        if top is None:
            top = mx
        eid = jnp.where(lane == kk, idx, eid)
        ex = jnp.where(lane == kk, jnp.exp(mx - top), ex)
        hits = hits + jnp.where(sel, 1.0, 0.0)
        logits = jnp.where(sel, -jnp.inf, logits)
    eid_ref[...] = eid.astype(I32)
    gate_ref[...] = ex / jnp.sum(ex, axis=-1, keepdims=True)
    cnt_ref[0:1, :] += jnp.sum(hits, axis=0, keepdims=True)


def out_proj(a, z, hf, hb, s, x, mod, w, g2, rw, rb, tiles, n_batch):
    t, d = x.shape
    tm = tiles["tm"]
    tpb = tiles["tpb"]
    row = lambda i: (i, 0)
    return pl.pallas_call(
        _out_proj_kernel,
        out_shape=(
            jax.ShapeDtypeStruct((t, d), F32),
            jax.ShapeDtypeStruct((t, d), F32),
            jax.ShapeDtypeStruct((t, LANES), I32),
            jax.ShapeDtypeStruct((t, LANES), F32),
            jax.ShapeDtypeStruct((8, LANES), F32),
        ),
        grid=(t // tm,),
        in_specs=[
            pl.BlockSpec((tm, MLA_WIDTH), row),
            pl.BlockSpec((tm, ML_WIDTH), lambda i: (i, Z_MLO // ML_WIDTH)),
            pl.BlockSpec((tm, ML_WIDTH), row),
            pl.BlockSpec((tm, ML_WIDTH), row),
            pl.BlockSpec((tm, SG_WIDTH), row),
            pl.BlockSpec((tm, d), row),
            pl.BlockSpec((None, 6, d), lambda i: (jnp.minimum(i // tpb, n_batch), 0, 0)),
            pl.BlockSpec(w.shape, lambda i: (0, 0)),
            pl.BlockSpec((1, d), lambda i: (0, 0)),
            pl.BlockSpec(rw.shape, lambda i: (0, 0)),
            pl.BlockSpec((1, LANES), lambda i: (0, 0)),
        ],
        out_specs=(
            pl.BlockSpec((tm, d), row),
            pl.BlockSpec((tm, d), row),
            pl.BlockSpec((tm, LANES), row),
            pl.BlockSpec((tm, LANES), row),
            pl.BlockSpec((8, LANES), lambda i: (0, 0)),
        ),
        compiler_params=_cparams("arbitrary"),
        name="out_proj",
    )(a, z, hf, hb, s, x, mod, w, g2, rw, rb)


def _moe_dest_kernel(eid_ref, base_ref, o_ref, run_scr, *, tm):
    @pl.when(pl.program_id(0) == 0)
    def _():
        run_scr[...] = jnp.zeros_like(run_scr)

    e = eid_ref[...]
    lane = lax.broadcasted_iota(I32, e.shape, 1)
    hot = [jnp.where(lane == e[:, kk:kk + 1], 1.0, 0.0) for kk in range(TOP_K)]
    hits = hot[0] + hot[1] + hot[2] + hot[3]
    r = lax.broadcasted_iota(I32, (tm, tm), 0)
    c = lax.broadcasted_iota(I32, (tm, tm), 1)
    before = jnp.where(c < r, 1.0, 0.0).astype(BF16)
    slot = jnp.dot(before, hits.astype(BF16), preferred_element_type=F32) + run_scr[0:1, :] + base_ref[...]
    out = jnp.zeros(e.shape, F32)
    for kk in range(TOP_K):
        out = jnp.where(lane == kk, jnp.sum(hot[kk] * slot, axis=-1, keepdims=True), out)
    o_ref[...] = out.astype(I32)
    run_scr[0:1, :] += jnp.sum(hits, axis=0, keepdims=True)


def moe_dest(eid, base):
    t = eid.shape[0]
    tm = _pick(t, (256, 128))
    return pl.pallas_call(
        functools.partial(_moe_dest_kernel, tm=tm),
        out_shape=jax.ShapeDtypeStruct((t, LANES), I32),
        grid=(t // tm,),
        in_specs=[pl.BlockSpec((tm, LANES), lambda i: (i, 0)), pl.BlockSpec((1, LANES), lambda i: (0, 0))],
        out_specs=pl.BlockSpec((tm, LANES), lambda i: (i, 0)),
        scratch_shapes=[pltpu.VMEM((8, LANES), F32)],
        compiler_params=_cparams("arbitrary"),
        name="moe_dest",
    )(eid, base)


def _moe_scatter_kernel(dest_ref, h_ref, xs_in_ref, xs_ref, sem, *, tm):
    del xs_in_ref

    def row_copy(t, slot):
        return pltpu.make_async_copy(h_ref.at[pl.ds(t, 1), :], xs_ref.at[pl.ds(slot, 1), :], sem)

    def start(t, carry):
        for kk in range(TOP_K):
            row_copy(t, dest_ref[t * TOP_K + kk]).start()
        return carry

    lax.fori_loop(0, tm, start, 0)

    def drain(t, carry):
        for kk in range(TOP_K):
            row_copy(0, 0).wait()
        return carry

    lax.fori_loop(0, tm, drain, 0)


def moe_scatter(dest_flat, h2, n_rows):
    t, d = h2.shape
    tm = _pick(t, (256, 128))
    xs0 = jnp.zeros((n_rows, d), F32)
    return pl.pallas_call(
        functools.partial(_moe_scatter_kernel, tm=tm),
        out_shape=jax.ShapeDtypeStruct((n_rows, d), F32),
        grid=(t // tm,),
        in_specs=[
            pl.BlockSpec((tm * TOP_K,), lambda i: (i,), memory_space=pltpu.SMEM),
            pl.BlockSpec((tm, d), lambda i: (i, 0)),
            pl.BlockSpec(memory_space=pl.ANY),
        ],
        out_specs=pl.BlockSpec(memory_space=pl.ANY),
        scratch_shapes=[pltpu.SemaphoreType.DMA(())],
        input_output_aliases={2: 0},
        compiler_params=_cparams("arbitrary"),
        name="moe_scatter",
    )(dest_flat, h2, xs0)


def _experts_kernel(be_ref, nb_ref, xs_ref, wgu_ref, bgu_ref, wd_ref, bd_ref, ys_ref):
    del be_ref

    @pl.when(pl.program_id(0) < nb_ref[0])
    def _():
        xb = xs_ref[...].astype(BF16)
        gu = jnp.dot(xb, wgu_ref[0], preferred_element_type=F32) + bgu_ref[0]
        gate = jnp.minimum(gu[:, :D_EXPERT], SWIGLU_LIMIT)
        up = jnp.clip(gu[:, D_EXPERT:], -SWIGLU_LIMIT, SWIGLU_LIMIT)
        act = (up + 1.0) * (gate * jax.nn.sigmoid(SWIGLU_ALPHA * gate))
        ys_ref[...] = jnp.dot(act.astype(BF16), wd_ref[0], preferred_element_type=F32) + bd_ref[0]

    @pl.when(pl.program_id(0) >= nb_ref[0])
    def _():
        ys_ref[...] = jnp.zeros_like(ys_ref)


def experts(block_exp, n_used, xs, wgu, bgu, wd, bd, blk):
    n_rows, d = xs.shape
    ne, _, f2 = wgu.shape
    f = f2 // 2
    grid_spec = pltpu.PrefetchScalarGridSpec(
        num_scalar_prefetch=2,
        grid=(n_rows // blk,),
        in_specs=[
            pl.BlockSpec((blk, d), lambda r, be, nb: (r, 0)),
            pl.BlockSpec((1, d, f2), lambda r, be, nb: (be[r], 0, 0)),
            pl.BlockSpec((1, 1, f2), lambda r, be, nb: (be[r], 0, 0)),
            pl.BlockSpec((1, f, d), lambda r, be, nb: (be[r], 0, 0)),
            pl.BlockSpec((1, 1, d), lambda r, be, nb: (be[r], 0, 0)),
        ],
        out_specs=pl.BlockSpec((blk, d), lambda r, be, nb: (r, 0)),
    )
    return pl.pallas_call(
        _experts_kernel,
        out_shape=jax.ShapeDtypeStruct((n_rows, d), F32),
        grid_spec=grid_spec,
        compiler_params=_cparams("arbitrary"),
        name="experts",
    )(block_exp, n_used, xs, wgu, bgu.reshape(ne, 1, f2), wd, bd.reshape(ne, 1, d))


def _moe_combine_kernel(dest_ref, gate_ref, x_ref, mod_ref, ys_ref, o_ref, buf, sem, *, tm):
    def row_copy(t, kk, slot):
        return pltpu.make_async_copy(ys_ref.at[pl.ds(slot, 1), :], buf.at[kk, pl.ds(t, 1), :], sem)

    def start(t, carry):
        for kk in range(TOP_K):
            row_copy(t, kk, dest_ref[t * TOP_K + kk]).start()
        return carry

    lax.fori_loop(0, tm, start, 0)

    def drain(t, carry):
        for kk in range(TOP_K):
            row_copy(0, 0, 0).wait()
        return carry

    lax.fori_loop(0, tm, drain, 0)

    gates = gate_ref[...]
    f = gates[:, 0:1] * buf[0]
    for kk in range(1, TOP_K):
        f = f + gates[:, kk:kk + 1] * buf[kk]
    o_ref[...] = x_ref[...] + mod_ref[5:6, :] * f


def moe_combine(dest_flat, gate, x, mod, ys, tiles, n_batch):
    t, d = x.shape
    tm = _pick(t, (256, 128))
    tpb = tiles["tpb"] * (tiles["tm"] // tm)
    return pl.pallas_call(
        functools.partial(_moe_combine_kernel, tm=tm),
        out_shape=jax.ShapeDtypeStruct((t, d), F32),
        grid=(t // tm,),
        in_specs=[
            pl.BlockSpec((tm * TOP_K,), lambda i: (i,), memory_space=pltpu.SMEM),
            pl.BlockSpec((tm, LANES), lambda i: (i, 0)),
            pl.BlockSpec((tm, d), lambda i: (i, 0)),
            pl.BlockSpec((None, 6, d), lambda i: (jnp.minimum(i // tpb, n_batch), 0, 0)),
            pl.BlockSpec(memory_space=pl.ANY),
        ],
        out_specs=pl.BlockSpec((tm, d), lambda i: (i, 0)),
        scratch_shapes=[pltpu.VMEM((TOP_K, tm, d), F32), pltpu.SemaphoreType.DMA(())],
        compiler_params=_cparams("arbitrary"),
        name="moe_combine",
    )(dest_flat, gate, x, mod, ys)


def _final_norm_kernel(x_ref, g_ref, o_ref):
    o_ref[...] = _rms(x_ref[...], g_ref[...])


def final_norm(x, g, t_lat):
    d = x.shape[1]
    tm = _pick(t_lat, (512, 256, 128))
    return pl.pallas_call(
        _final_norm_kernel,
        out_shape=jax.ShapeDtypeStruct((t_lat, d), F32),
        grid=(t_lat // tm,),
        in_specs=[pl.BlockSpec((tm, d), lambda i: (i, 0)), pl.BlockSpec((1, d), lambda i: (0, 0))],
        out_specs=pl.BlockSpec((tm, d), lambda i: (i, 0)),
        compiler_params=_cparams("parallel"),
        name="final_norm",
    )(x, g)


_SWAP64 = tuple(list(range(16, 32)) + list(range(0, 16)) + list(range(48, 64)) + list(range(32, 48)))


def _prep_w_in(w):
    d = w.shape[0]
    n_mla = Q_LORA + KV_LORA + QK_ROPE
    ml0 = n_mla
    gate0 = ml0 + 4 * ML_WIDTH
    sg0 = gate0 + 4 * ML_HEADS
    kpe = w[:, Q_LORA + KV_LORA:n_mla]
    parts = [
        w[:, :Q_LORA + KV_LORA], kpe, kpe[:, jnp.array(_SWAP64)],
        w[:, gate0:sg0], jnp.zeros((d, LANES - 4 * ML_HEADS), w.dtype),
        w[:, ml0:gate0], w[:, sg0:],
    ]
    return jnp.concatenate(parts, axis=1).astype(BF16)


def _prep_w_q(w):
    k = w.shape[0]
    w = w.reshape(k, MLA_HEADS, QK_NOPE + QK_ROPE)
    pe = w[..., QK_NOPE:]
    zeros = jnp.zeros((k, MLA_HEADS, LANES - QK_ROPE), w.dtype)
    out = jnp.concatenate([w[..., :QK_NOPE], pe, zeros, pe[..., jnp.array(_SWAP64)], zeros], axis=-1)
    return out.reshape(k, MLA_HEADS * 3 * LANES).astype(BF16)


def _rope_tables(n_batch, seq, ctx_len):
    rows = seq // GRID_W
    row = jnp.repeat(jnp.arange(rows), GRID_W).astype(F32)
    col = jnp.tile(jnp.arange(GRID_W), rows).astype(F32)
    n_freq = QK_ROPE // 4
    inv = ROPE_THETA ** (-jnp.arange(n_freq, dtype=F32) / n_freq)
    ra = row[:, None] * inv[None, :]
    ca = col[:, None] * inv[None, :]
    pad = jnp.zeros((seq, LANES - QK_ROPE), F32)
    cs = jnp.concatenate([jnp.cos(ra), jnp.cos(ra), jnp.cos(ca), jnp.cos(ca), pad], axis=1)
    sn = jnp.concatenate([-jnp.sin(ra), jnp.sin(ra), -jnp.sin(ca), jnp.sin(ca), pad], axis=1)
    n_ctx = n_batch * ctx_len
    cs_ctx = jnp.concatenate([jnp.ones((n_ctx, QK_ROPE), F32), jnp.zeros((n_ctx, LANES - QK_ROPE), F32)], axis=1)
    cs = jnp.concatenate([jnp.tile(cs, (n_batch, 1)), cs_ctx], axis=0)
    sn = jnp.concatenate([jnp.tile(sn, (n_batch, 1)), jnp.zeros((n_ctx, LANES), F32)], axis=0)
    return cs, sn


def _moe_plan(counts, n_blocks, blk):
    counts = counts.astype(I32)
    padded = (counts + blk - 1) // blk * blk
    pad_end = jnp.cumsum(padded)
    pad_start = pad_end - padded
    block_exp = jnp.minimum(
        jnp.searchsorted(pad_end, jnp.arange(n_blocks, dtype=I32) * blk, side="right"), N_EXPERTS - 1).astype(I32)
    n_used = (pad_end[-1] // blk).astype(I32).reshape(1)
    base = jnp.zeros((1, LANES), F32).at[0, :N_EXPERTS].set(pad_start.astype(F32))
    return base, block_exp, n_used


def kernel(x, c, ctx, c_ctx, norm1, norm2, w_ada, b_ada, w_in, q_a_norm, w_q_b, kv_a_norm, w_kv_b, ml_conv_w, ml_conv_b, ml_gate_b, sg_norm, sg_w, sg_b, w_out, router_w, router_b, w_gu, b_gu, w_down, b_down, final_norm_g):
    n_batch, seq, d = x.shape
    ctx_len = ctx.shape[1]
    depth = w_in.shape[0]
    t_lat = n_batch * seq
    t_all = t_lat + n_batch * ctx_len

    xa = jnp.concatenate([x.reshape(t_lat, d), ctx.reshape(n_batch * ctx_len, d)], axis=0)

    cond = jnp.concatenate([c, c_ctx[None, :], jnp.zeros((8 - n_batch - 1, d), F32)], axis=0)
    mod_all = ada_all(cond, w_ada, b_ada).reshape(depth, 8, 6, d)

    cs, sn = _rope_tables(n_batch, seq, ctx_len)

    tm_big = _pick(ctx_len * n_batch, (512, 256, 128))
    tm_big = _pick(seq, (tm_big,))
    tiles_in = {"tm": tm_big, "tpb": seq // tm_big}
    tm_out = _pick(ctx_len, (256, 128))
    tiles_out = {"tm": tm_out, "tpb": seq // tm_out}

    blk = 256
    n_blocks = -(-(t_all * TOP_K) // blk) + N_EXPERTS
    n_rows = n_blocks * blk

    for l in range(depth):
        mod = mod_all[l]
        z = in_proj(xa, mod, norm1[l][None, :], _prep_w_in(w_in[l]), tiles_in, n_batch)

        q, k, v = mla_proj(z, q_a_norm[l][None, :], kv_a_norm[l][None, :], _prep_w_q(w_q_b[l]),
                           w_kv_b[l].astype(BF16), cs, sn)
        a_lat = attn_latent(q, k, v, n_batch, seq, ctx_len)
        a_ctx = attn_context(q, k, v, n_batch, seq, ctx_len)
        a = jnp.concatenate([a_lat, a_ctx], axis=0)

        qk = ml_prep(z, ml_conv_w[l], ml_conv_b[l][None, :], seq, ctx_len, t_lat)
        gate_b = jnp.zeros((1, LANES), F32).at[0, :4 * ML_HEADS].set(ml_gate_b[l])
        hf, hb = mlstm(qk, z, gate_b, n_batch, seq, ctx_len)

        sg_b_bc = jnp.broadcast_to(sg_b[l][:, :, None], (SG_GROUPS, SG_CHUNK, SG_DIM))
        s = sgu(z, sg_norm[l][None, :], sg_w[l].astype(BF16), sg_b_bc)

        rw = jnp.zeros((d, LANES), BF16).at[:, :N_EXPERTS].set(router_w[l].astype(BF16))
        rb = jnp.full((1, LANES), NEG_BIG, F32).at[0, :N_EXPERTS].set(router_b[l])
        xm, h2, eid, gate, counts = out_proj(a, z, hf, hb, s, xa, mod, w_out[l].astype(BF16),
                                             norm2[l][None, :], rw, rb, tiles_out, n_batch)

        base, block_exp, n_used = _moe_plan(counts[0, :N_EXPERTS], n_blocks, blk)
        dest = moe_dest(eid, base)
        dest_flat = dest[:, :TOP_K].reshape(t_all * TOP_K)
        xs = moe_scatter(dest_flat, h2, n_rows)
        wgu = jnp.concatenate([w_gu[l][:, :, 0::2], w_gu[l][:, :, 1::2]], axis=-1).astype(BF16)
        bgu = jnp.concatenate([b_gu[l][:, 0::2], b_gu[l][:, 1::2]], axis=-1)
        ys = experts(block_exp, n_used, xs, wgu, bgu, w_down[l].astype(BF16), b_down[l], blk)
        xa = moe_combine(dest_flat, gate, xm, mod, ys, tiles_out, n_batch)

    return final_norm(xa, final_norm_g[None, :], t_lat).reshape(n_batch, seq, d)
```
